```python
import jax, jax.numpy as jnp
from jax import lax
import numpy as np

D_MODEL = 1024
BATCH = 4
SEQ = 4096
DEPTH = 1
DEC_BATCH = 16
DEC_SEQ = 16
PAST_LEN = 2048

CHUNK = 64
D_MIX = D_MODEL
D_POOL = D_MIX // 2
D_CONV = D_MIX - D_POOL
POOL_WINDOWS = (2, 4, 8, 16)
N_POOL_GROUPS = len(POOL_WINDOWS)
POOL_GROUP = D_POOL // N_POOL_GROUPS
POOL_STATE = max(POOL_WINDOWS) - 1
N_CONV_HEADS = 8
CONV_WIDTH = 3
CONV_STATE = CONV_WIDTH - 1
D_IN = D_POOL + 3 * D_CONV
N_MEM = 256
N_XHEADS = 4
XHEAD_DIM = D_MODEL // N_XHEADS
D_FF = 4 * D_MODEL
EPS = 1e-6

kernel_name = 'hybrid_pool_conv_stream_encoder_step'


def rmsnorm(x, g):
    x32 = x.astype(jnp.float32)
    y = x32 * lax.rsqrt(jnp.mean(jnp.square(x32), axis=-1, keepdims=True) + EPS)
    return (y * g.astype(jnp.float32)).astype(x.dtype)


def group_rmsnorm(y, g, n_groups):
    b, t, d = y.shape
    y32 = y.astype(jnp.float32).reshape(b, t, n_groups, d // n_groups)
    y32 = y32 * lax.rsqrt(jnp.mean(jnp.square(y32), axis=-1, keepdims=True) + EPS)
    return (y32.reshape(b, t, d) * g.astype(jnp.float32)).astype(y.dtype)


def pool_mix(u_ext, start_pos, w_pool, pool_scale):
    b, l, _ = u_ext.shape
    t = l - POOL_STATE
    u32 = u_ext.astype(jnp.float32)
    cs = jnp.cumsum(jnp.pad(u32, ((0, 0), (1, 0), (0, 0))), axis=1)
    pos = start_pos + jnp.arange(t, dtype=jnp.int32)
    lo = POOL_STATE + 1
    outs = []
    for gi, w in enumerate(POOL_WINDOWS):
        sl = slice(gi * POOL_GROUP, (gi + 1) * POOL_GROUP)
        win_sum = cs[:, lo:lo + t, sl] - cs[:, lo - w:lo - w + t, sl]
        cnt = jnp.minimum(pos + 1, w).astype(jnp.float32)[None, :, None]
        outs.append(win_sum / cnt - u32[:, POOL_STATE:, sl])
    pooled = jnp.stack(outs, axis=2)
    mapped = jnp.einsum('btgc,gcd->btgd', pooled, w_pool.astype(jnp.float32))
    return (mapped.reshape(b, t, D_POOL) * pool_scale.astype(jnp.float32)).astype(u_ext.dtype)


def conv_mix(bg, cg, h, conv_prev, w_conv):
    v = cg * h
    v_ext = jnp.concatenate([conv_prev.astype(v.dtype), v], axis=1)
    t = v.shape[1]
    z = sum(w_conv[k] * v_ext[:, k:k + t] for k in range(CONV_WIDTH))
    return bg * z, v_ext[:, -CONV_STATE:]


def memory_kv(mem, g_mem, w_k, w_v):
    m = rmsnorm(mem, g_mem)
    k = jnp.einsum('bmd,dhe->bmhe', m, w_k)
    v = jnp.einsum('bmd,dhe->bmhe', m, w_v)
    return k, v


def cross_attn(xn, mem_k, mem_v, w_q, w_co):
    q = jnp.einsum('btd,dhe->bthe', xn, w_q)
    s = jnp.einsum('bthe,bmhe->bhtm', q, mem_k).astype(jnp.float32) * (XHEAD_DIM ** -0.5)
    p = jax.nn.softmax(s, axis=-1).astype(mem_v.dtype)
    o = jnp.einsum('bhtm,bmhe->bthe', p, mem_v)
    return jnp.einsum('bthe,hed->btd', o, w_co)


def encoder_layer(x, pool_prev, conv_prev, start_pos, mem_k, mem_v,
                  g_mix_pre, w_in, w_pool, pool_scale, w_conv, g_pool_out, g_conv_out, w_out,
                  g_mix_post, g_x_pre, w_q, w_co, g_x_post, g_ff_pre, w_up, w_down, g_ff_post):
    xn = rmsnorm(x, g_mix_pre)
    proj = xn @ w_in
    u, bg, cg, h = jnp.split(proj, [D_POOL, D_POOL + D_CONV, D_POOL + 2 * D_CONV], axis=-1)
    u_ext = jnp.concatenate([pool_prev.astype(u.dtype), u], axis=1)
    y_pool = pool_mix(u_ext, start_pos, w_pool, pool_scale)
    y_conv, conv_state = conv_mix(bg, cg, h, conv_prev, w_conv)
    merged = jnp.concatenate([group_rmsnorm(y_pool, g_pool_out, N_POOL_GROUPS),
                              group_rmsnorm(y_conv, g_conv_out, N_CONV_HEADS)], axis=-1)
    x = x + rmsnorm(merged @ w_out, g_mix_post)
    x = x + rmsnorm(cross_attn(rmsnorm(x, g_x_pre), mem_k, mem_v, w_q, w_co), g_x_post)
    hid = jnp.square(jax.nn.relu(rmsnorm(x, g_ff_pre) @ w_up))
    x = x + rmsnorm(hid @ w_down, g_ff_post)
    return x, u_ext[:, -POOL_STATE:], conv_state


def setup_inputs(seed: int = 0) -> dict:
    key = jax.random.key(seed)
    ks = iter(jax.random.split(key, 40))

    def nrm(shape, scale):
        return jax.random.normal(next(ks), shape, jnp.float32) * scale

    def gain(shape):
        return 1.0 + nrm(shape, 0.05)

    L = DEPTH
    return {
        'x_prompt': nrm((BATCH, SEQ, D_MODEL), 1.0),
        'x_sample': nrm((DEC_BATCH, DEC_SEQ, D_MODEL), 1.0),
        'state_pool': nrm((L, DEC_BATCH, POOL_STATE, D_POOL), 1.0),
        'state_conv': nrm((L, DEC_BATCH, CONV_STATE, D_CONV), 1.0),
        'cache_mem_k': nrm((L, DEC_BATCH, N_MEM, N_XHEADS, XHEAD_DIM), 1.0),
        'cache_mem_v': nrm((L, DEC_BATCH, N_MEM, N_XHEADS, XHEAD_DIM), 1.0),
        'mem_prompt': nrm((BATCH, N_MEM, D_MODEL), 1.0),
        'g_mix_pre': gain((L, D_MODEL)),
        'w_in': nrm((L, D_MODEL, D_IN), D_MODEL ** -0.5),
        'w_pool': nrm((L, N_POOL_GROUPS, POOL_GROUP, POOL_GROUP), POOL_GROUP ** -0.5),
        'pool_scale': 0.5 + nrm((L, D_POOL), 0.1),
        'w_conv': nrm((L, CONV_WIDTH, D_CONV), CONV_WIDTH ** -0.5),
        'g_pool_out': gain((L, D_POOL)),
        'g_conv_out': gain((L, D_CONV)),
        'w_out': nrm((L, D_MIX, D_MODEL), D_MIX ** -0.5),
        'g_mix_post': gain((L, D_MODEL)),
        'g_mem': gain((L, D_MODEL)),
        'w_k': nrm((L, D_MODEL, N_XHEADS, XHEAD_DIM), D_MODEL ** -0.5),
        'w_v': nrm((L, D_MODEL, N_XHEADS, XHEAD_DIM), D_MODEL ** -0.5),
        'g_x_pre': gain((L, D_MODEL)),
        'w_q': nrm((L, D_MODEL, N_XHEADS, XHEAD_DIM), D_MODEL ** -0.5),
        'w_co': nrm((L, N_XHEADS, XHEAD_DIM, D_MODEL), D_MODEL ** -0.5),
        'g_x_post': gain((L, D_MODEL)),
        'g_ff_pre': gain((L, D_MODEL)),
        'w_up': nrm((L, D_MODEL, D_FF), D_MODEL ** -0.5),
        'w_down': nrm((L, D_FF, D_MODEL), D_FF ** -0.5),
        'g_ff_post': gain((L, D_MODEL)),
    }


def reference(x_prompt, x_sample, state_pool, state_conv, cache_mem_k, cache_mem_v, mem_prompt,
              g_mix_pre, w_in, w_pool, pool_scale, w_conv, g_pool_out, g_conv_out, w_out,
              g_mix_post, g_mem, w_k, w_v, g_x_pre, w_q, w_co, g_x_post,
              g_ff_pre, w_up, w_down, g_ff_post):
    yp, ys = x_prompt, x_sample
    bp = x_prompt.shape[0]
    pool_p, conv_p, mk_p, mv_p, pool_s, conv_s = [], [], [], [], [], []
    for l in range(DEPTH):
        lw = (g_mix_pre[l], w_in[l], w_pool[l], pool_scale[l], w_conv[l], g_pool_out[l],
              g_conv_out[l], w_out[l], g_mix_post[l], g_x_pre[l], w_q[l], w_co[l], g_x_post[l],
              g_ff_pre[l], w_up[l], w_down[l], g_ff_post[l])
        mk, mv = memory_kv(mem_prompt, g_mem[l], w_k[l], w_v[l])
        zero_pool = jnp.zeros((bp, POOL_STATE, D_POOL), yp.dtype)
        zero_conv = jnp.zeros((bp, CONV_STATE, D_CONV), yp.dtype)
        yp, sp, cp = encoder_layer(yp, zero_pool, zero_conv, 0, mk, mv, *lw)
        ys, ss, cs = encoder_layer(ys, state_pool[l], state_conv[l], PAST_LEN,
                                   cache_mem_k[l], cache_mem_v[l], *lw)
        pool_p.append(sp); conv_p.append(cp); mk_p.append(mk); mv_p.append(mv)
        pool_s.append(ss); conv_s.append(cs)
    return (yp, ys, jnp.stack(pool_p), jnp.stack(conv_p), jnp.stack(mk_p), jnp.stack(mv_p),
            jnp.stack(pool_s), jnp.stack(conv_s))
```

```python
import functools

import jax
import jax.numpy as jnp
from jax import lax
from jax.experimental import pallas as pl
from jax.experimental.pallas import tpu as pltpu

EPS = 1e-6
POOL_WINDOWS = (2, 4, 8, 16)
N_POOL_GROUPS = len(POOL_WINDOWS)
POOL_STATE = max(POOL_WINDOWS) - 1
N_CONV_HEADS = 8
CONV_WIDTH = 3
CONV_STATE = CONV_WIDTH - 1
PAST_LEN = 2048

POOL_HIST = 16
CONV_HIST = 8

VMEM_LIMIT_BYTES = 56 * 1024 * 1024

F32 = jnp.float32
BF16 = jnp.bfloat16


def _rms(x, g):
    ms = jnp.mean(x * x, axis=-1, keepdims=True)
    return x * lax.rsqrt(ms + EPS) * g


def _dot(a, b):
    return jnp.dot(a, b, preferred_element_type=F32)


def _const_spec(shape):
    return pl.BlockSpec(shape, lambda *_: (0,) * len(shape))


def _mixer_kernel(x_ref, ph_ref, ch_ref, gpre_ref, win_ref, wpool_ref, pscale_ref, wconv_ref,
                  gpo_ref, gco_ref, hmean_ref, wout_ref, gpost_ref,
                  y_ref, pst_ref, cst_ref, ubuf, vbuf, *, start_pos, nb, tm):
    d = x_ref.shape[-1]
    dp = ubuf.shape[-1]
    dc = vbuf.shape[-1]
    pg = dp // N_POOL_GROUPS
    m = nb * tm
    t = pl.program_id(1)

    @pl.when(t == 0)
    def _():
        ubuf[:, 0:POOL_HIST, :] = ph_ref[...]
        vbuf[:, 0:CONV_HIST, :] = ch_ref[...]

    x = x_ref[...].reshape(m, d)
    xn = _rms(x, gpre_ref[...]).astype(BF16)
    proj = _dot(xn, win_ref[...])
    u = proj[:, :dp]
    bg = proj[:, dp:dp + dc]
    cg = proj[:, dp + dc:dp + 2 * dc]
    h = proj[:, dp + 2 * dc:]
    v = cg * h
    ubuf[:, POOL_HIST:POOL_HIST + tm, :] = u.reshape(nb, tm, dp)
    vbuf[:, CONV_HIST:CONV_HIST + tm, :] = v.reshape(nb, tm, dc)

    pos = start_pos + t * tm + lax.broadcasted_iota(jnp.int32, (1, tm, 1), 1)
    pooled = []
    for gi, w in enumerate(POOL_WINDOWS):
        sl = slice(gi * pg, (gi + 1) * pg)
        cur = ubuf[:, POOL_HIST:POOL_HIST + tm, sl]
        win = cur
        for k in range(1, w):
            win = win + ubuf[:, POOL_HIST - k:POOL_HIST - k + tm, sl]
        inv_cnt = 1.0 / jnp.minimum(pos + 1, w).astype(F32)
        pooled.append((win * inv_cnt - cur).reshape(m, pg))
    mapped = []
    for j in range(N_POOL_GROUPS // 2):
        pj = jnp.concatenate(pooled[2 * j:2 * j + 2], axis=-1).astype(BF16)
        mapped.append(_dot(pj, wpool_ref[j]))
    y_pool = jnp.concatenate(mapped, axis=-1) * pscale_ref[...]
    pn = []
    for gi in range(N_POOL_GROUPS):
        sl = slice(gi * pg, (gi + 1) * pg)
        pn.append(_rms(y_pool[:, sl], gpo_ref[:, sl]))

    z = wconv_ref[0:1, :] * vbuf[:, CONV_HIST - 2:CONV_HIST - 2 + tm, :]
    z = z + wconv_ref[1:2, :] * vbuf[:, CONV_HIST - 1:CONV_HIST - 1 + tm, :]
    z = z + wconv_ref[2:3, :] * vbuf[:, CONV_HIST:CONV_HIST + tm, :]
    y_conv = bg * z.reshape(m, dc)
    ms = _dot((y_conv * y_conv).astype(BF16), hmean_ref[...])
    cn = y_conv * lax.rsqrt(ms + EPS) * gco_ref[...]

    merged = jnp.concatenate(pn + [cn], axis=-1).astype(BF16)
    y = x + _rms(_dot(merged, wout_ref[...]), gpost_ref[...])
    y_ref[...] = y.reshape(nb, tm, d)

    ubuf[:, 0:POOL_HIST, :] = ubuf[:, tm:tm + POOL_HIST, :]
    vbuf[:, 0:CONV_HIST, :] = vbuf[:, tm:tm + CONV_HIST, :]

    @pl.when(t == pl.num_programs(1) - 1)
    def _():
        pst_ref[...] = ubuf[:, POOL_HIST - POOL_STATE:POOL_HIST, :]
        cst_ref[...] = vbuf[:, CONV_HIST - CONV_STATE:CONV_HIST, :]


def _mixer(x, pool_hist, conv_hist, start_pos, nb, tm, w):
    b, t, d = x.shape
    dp = pool_hist.shape[-1]
    dc = conv_hist.shape[-1]
    assert b % nb == 0 and t % tm == 0 and tm % POOL_HIST == 0
    ph = jnp.pad(pool_hist, ((0, 0), (POOL_HIST - POOL_STATE, 0), (0, 0)))
    ch = jnp.pad(conv_hist, ((0, 0), (CONV_HIST - CONV_STATE, 0), (0, 0)))
    kern = functools.partial(_mixer_kernel, start_pos=start_pos, nb=nb, tm=tm)
    return pl.pallas_call(
        kern,
        grid=(b // nb, t // tm),
        in_specs=[
            pl.BlockSpec((nb, tm, d), lambda i, j: (i, j, 0)),
            pl.BlockSpec((nb, POOL_HIST, dp), lambda i, j: (i, 0, 0)),
            pl.BlockSpec((nb, CONV_HIST, dc), lambda i, j: (i, 0, 0)),
            _const_spec((1, d)),
            _const_spec(w['w_in'].shape),
            _const_spec(w['w_pool'].shape),
            _const_spec((1, dp)),
            _const_spec((CONV_WIDTH, dc)),
            _const_spec((1, dp)),
            _const_spec((1, dc)),
            _const_spec((dc, dc)),
            _const_spec(w['w_out'].shape),
            _const_spec((1, d)),
        ],
        out_specs=[
            pl.BlockSpec((nb, tm, d), lambda i, j: (i, j, 0)),
            pl.BlockSpec((nb, POOL_STATE, dp), lambda i, j: (i, 0, 0)),
            pl.BlockSpec((nb, CONV_STATE, dc), lambda i, j: (i, 0, 0)),
        ],
        out_shape=[
            jax.ShapeDtypeStruct((b, t, d), F32),
            jax.ShapeDtypeStruct((b, POOL_STATE, dp), F32),
            jax.ShapeDtypeStruct((b, CONV_STATE, dc), F32),
        ],
        scratch_shapes=[
            pltpu.VMEM((nb, POOL_HIST + tm, dp), F32),
            pltpu.VMEM((nb, CONV_HIST + tm, dc), F32),
        ],
        compiler_params=pltpu.CompilerParams(
            dimension_semantics=("arbitrary", "arbitrary"), vmem_limit_bytes=VMEM_LIMIT_BYTES),
        name="mixer",
    )(x, ph, ch, w['g_mix_pre'], w['w_in'], w['w_pool'], w['pool_scale'], w['w_conv'],
      w['g_pool_out'], w['g_conv_out'], w['head_mean'], w['w_out'], w['g_mix_post'])


def _memkv_kernel(mem_ref, g_ref, wk_ref, wv_ref, k_ref, v_ref):
    mn = _rms(mem_ref[0], g_ref[...]).astype(BF16)
    k_ref[0] = _dot(mn, wk_ref[...])
    v_ref[0] = _dot(mn, wv_ref[...])


def _memory_kv(mem, w):
    b, n, d = mem.shape
    blk = pl.BlockSpec((1, n, d), lambda i: (i, 0, 0))
    return pl.pallas_call(
        _memkv_kernel,
        grid=(b,),
        in_specs=[blk, _const_spec((1, d)), _const_spec((d, d)), _const_spec((d, d))],
        out_specs=[blk, blk],
        out_shape=[jax.ShapeDtypeStruct((b, n, d), F32)] * 2,
        compiler_params=pltpu.CompilerParams(
            dimension_semantics=("arbitrary",), vmem_limit_bytes=VMEM_LIMIT_BYTES),
        name="memory_kv",
    )(mem, w['g_mem'], w['w_k'], w['w_v'])


def _attn_kernel(x_ref, k_ref, v_ref, gpre_ref, wq_ref, wco_ref, gpost_ref, y_ref, kb, vb,
                 *, nb, tm, n_heads):
    d = x_ref.shape[-1]
    e = d // n_heads
    m = nb * tm

    @pl.when(pl.program_id(1) == 0)
    def _():
        kb[...] = k_ref[...].astype(BF16)
        vb[...] = v_ref[...].astype(BF16)

    x = x_ref[...].reshape(m, d)
    xn = _rms(x, gpre_ref[...]).astype(BF16)
    q = (_dot(xn, wq_ref[...]) * (e ** -0.5)).astype(BF16)
    rows = []
    for i in range(nb):
        heads = []
        for hh in range(n_heads):
            sl = slice(hh * e, (hh + 1) * e)
            s = lax.dot_general(q[i * tm:(i + 1) * tm, sl], kb[i, :, sl],
                                (((1,), (1,)), ((), ())), preferred_element_type=F32)
            p = jnp.exp(s - jnp.max(s, axis=-1, keepdims=True))
            p = p * (1.0 / jnp.sum(p, axis=-1, keepdims=True))
            heads.append(_dot(p.astype(BF16), vb[i, :, sl]))
        rows.append(jnp.concatenate(heads, axis=-1))
    o = (rows[0] if nb == 1 else jnp.concatenate(rows, axis=0)).astype(BF16)
    y = x + _rms(_dot(o, wco_ref[...]), gpost_ref[...])
    y_ref[...] = y.reshape(nb, tm, d)


def _attn(x, mem_k, mem_v, nb, tm, n_heads, w):
    b, t, d = x.shape
    n = mem_k.shape[1]
    assert b % nb == 0 and t % tm == 0
    kern = functools.partial(_attn_kernel, nb=nb, tm=tm, n_heads=n_heads)
    xblk = pl.BlockSpec((nb, tm, d), lambda i, j: (i, j, 0))
    kvblk = pl.BlockSpec((nb, n, d), lambda i, j: (i, 0, 0))
    return pl.pallas_call(
        kern,
        grid=(b // nb, t // tm),
        in_specs=[xblk, kvblk, kvblk, _const_spec((1, d)), _const_spec((d, d)),
                  _const_spec((d, d)), _const_spec((1, d))],
        out_specs=xblk,
        out_shape=jax.ShapeDtypeStruct((b, t, d), F32),
        scratch_shapes=[pltpu.VMEM((nb, n, d), BF16), pltpu.VMEM((nb, n, d), BF16)],
        compiler_params=pltpu.CompilerParams(
            dimension_semantics=("arbitrary", "arbitrary"), vmem_limit_bytes=VMEM_LIMIT_BYTES),
        name="cross_attn",
    )(x, mem_k, mem_v, w['g_x_pre'], w['w_q'], w['w_co'], w['g_x_post'])


def _mlp_kernel(x_ref, gpre_ref, wup_ref, wdown_ref, gpost_ref, y_ref, *, fc):
    x = x_ref[...]
    xn = _rms(x, gpre_ref[...]).astype(BF16)
    dff = wup_ref.shape[-1]
    acc = None
    for c in range(dff // fc):
        hid = jnp.square(jnp.maximum(_dot(xn, wup_ref[:, c * fc:(c + 1) * fc]), 0.0))
        part = _dot(hid.astype(BF16), wdown_ref[c * fc:(c + 1) * fc, :])
        acc = part if acc is None else acc + part
    y_ref[...] = x + _rms(acc, gpost_ref[...])


def _mlp(x2d, tm, fc, w):
    m, d = x2d.shape
    dff = w['w_up'].shape[-1]
    assert m % tm == 0 and dff % fc == 0
    xblk = pl.BlockSpec((tm, d), lambda i: (i, 0))
    return pl.pallas_call(
        functools.partial(_mlp_kernel, fc=fc),
        grid=(m // tm,),
        in_specs=[xblk, _const_spec((1, d)), _const_spec((d, dff)), _const_spec((dff, d)),
                  _const_spec((1, d))],
        out_specs=xblk,
        out_shape=jax.ShapeDtypeStruct((m, d), F32),
        compiler_params=pltpu.CompilerParams(
            dimension_semantics=("arbitrary",), vmem_limit_bytes=VMEM_LIMIT_BYTES),
        name="mlp",
    )(x2d, w['g_ff_pre'], w['w_up'], w['w_down'], w['g_ff_post'])


def _layer(x, pool_hist, conv_hist, start_pos, mem_k, mem_v, w, *, mix_blk, attn_blk, mlp_tm):
    b, t, d = x.shape
    n_heads = w['n_heads']
    x, pst, cst = _mixer(x, pool_hist, conv_hist, start_pos, mix_blk[0], mix_blk[1], w)
    x = _attn(x, mem_k.reshape(b, -1, d), mem_v.reshape(b, -1, d), attn_blk[0], attn_blk[1],
              n_heads, w)
    x = _mlp(x.reshape(b * t, d), mlp_tm, 1024, w).reshape(b, t, d)
    return x, pst, cst


def _prep_weights(l, g_mix_pre, w_in, w_pool, pool_scale, w_conv, g_pool_out, g_conv_out, w_out,
                  g_mix_post, g_mem, w_k, w_v, g_x_pre, w_q, w_co, g_x_post, g_ff_pre, w_up,
                  w_down, g_ff_post):
    d = w_in.shape[1]
    n_heads = w_q.shape[2]
    pg = w_pool.shape[-1]
    dc = w_conv.shape[-1]
    wp = w_pool[l].astype(BF16)
    zero = jnp.zeros((pg, pg), BF16)
    wpool_bd = jnp.stack([
        jnp.block([[wp[2 * j], zero], [zero, wp[2 * j + 1]]]) for j in range(N_POOL_GROUPS // 2)])
    head = jnp.arange(dc, dtype=jnp.int32) // (dc // N_CONV_HEADS)
    head_mean = jnp.where(head[:, None] == head[None, :], N_CONV_HEADS / dc, 0.0).astype(BF16)
    row = lambda a: a[l].reshape(1, -1)
    return {
        'n_heads': n_heads,
        'g_mix_pre': row(g_mix_pre), 'w_in': w_in[l].astype(BF16), 'w_pool': wpool_bd,
        'pool_scale': row(pool_scale), 'w_conv': w_conv[l], 'g_pool_out': row(g_pool_out),
        'g_conv_out': row(g_conv_out), 'head_mean': head_mean, 'w_out': w_out[l].astype(BF16),
        'g_mix_post': row(g_mix_post), 'g_mem': row(g_mem),
        'w_k': w_k[l].reshape(d, d).astype(BF16), 'w_v': w_v[l].reshape(d, d).astype(BF16),
        'g_x_pre': row(g_x_pre), 'w_q': w_q[l].reshape(d, d).astype(BF16),
        'w_co': w_co[l].reshape(d, d).astype(BF16), 'g_x_post': row(g_x_post),
        'g_ff_pre': row(g_ff_pre), 'w_up': w_up[l].astype(BF16), 'w_down': w_down[l].astype(BF16),
        'g_ff_post': row(g_ff_post),
    }


def kernel(x_prompt, x_sample, state_pool, state_conv, cache_mem_k, cache_mem_v, mem_prompt, g_mix_pre, w_in, w_pool, pool_scale, w_conv, g_pool_out, g_conv_out, w_out, g_mix_post, g_mem, w_k, w_v, g_x_pre, w_q, w_co, g_x_post, g_ff_pre, w_up, w_down, g_ff_post):
    depth = w_in.shape[0]
    bp, sp, d = x_prompt.shape
    bs, ss, _ = x_sample.shape
    dp = state_pool.shape[-1]
    dc = state_conv.shape[-1]
    n_mem, n_heads, e = cache_mem_k.shape[2:]
    yp, ys = x_prompt, x_sample
    pool_p, conv_p, mk_p, mv_p, pool_s, conv_s = [], [], [], [], [], []
    for l in range(depth):
        w = _prep_weights(l, g_mix_pre, w_in, w_pool, pool_scale, w_conv, g_pool_out, g_conv_out,
                          w_out, g_mix_post, g_mem, w_k, w_v, g_x_pre, w_q, w_co, g_x_post,
                          g_ff_pre, w_up, w_down, g_ff_post)
        mk, mv = _memory_kv(mem_prompt, w)
        yp, pst, cst = _layer(yp, jnp.zeros((bp, POOL_STATE, dp), F32),
                              jnp.zeros((bp, CONV_STATE, dc), F32), 0, mk, mv, w,
                              mix_blk=(1, 512), attn_blk=(1, 512), mlp_tm=512)
        ys, pss, css = _layer(ys, state_pool[l], state_conv[l], PAST_LEN,
                              cache_mem_k[l], cache_mem_v[l], w,
                              mix_blk=(bs, ss), attn_blk=(4, ss), mlp_tm=bs * ss)
        pool_p.append(pst); conv_p.append(cst)
        mk_p.append(mk.reshape(bp, n_mem, n_heads, e)); mv_p.append(mv.reshape(bp, n_mem, n_heads, e))
        pool_s.append(pss); conv_s.append(css)
    return (yp, ys, jnp.stack(pool_p), jnp.stack(conv_p), jnp.stack(mk_p), jnp.stack(mv_p),
            jnp.stack(pool_s), jnp.stack(conv_s))
```

```python
import functools

import jax
import jax.numpy as jnp
from jax import lax
from jax.experimental import pallas as pl
from jax.experimental.pallas import tpu as pltpu

EPS = 1e-6
POOL_WINDOWS = (2, 4, 8, 16)
N_POOL_GROUPS = len(POOL_WINDOWS)
POOL_STATE = max(POOL_WINDOWS) - 1
N_CONV_HEADS = 8
CONV_WIDTH = 3
CONV_STATE = CONV_WIDTH - 1
PAST_LEN = 2048

POOL_HIST = 16
CONV_HIST = 8

PROMPT_TILE = 512
MLP_CHUNK = 1024
VMEM_LIMIT_BYTES = 56 * 1024 * 1024

F32 = jnp.float32
BF16 = jnp.bfloat16


def _rms(x, g):
    ms = jnp.mean(x * x, axis=-1, keepdims=True)
    return x * lax.rsqrt(ms + EPS) * g


def _dot(a, b):
    return jnp.dot(a, b, preferred_element_type=F32)


def _dot_nt(a, b):
    return lax.dot_general(a, b, (((1,), (1,)), ((), ())), preferred_element_type=F32)


def _const_spec(shape):
    return pl.BlockSpec(shape, lambda *_: (0,) * len(shape))


def _params():
    return pltpu.CompilerParams(dimension_semantics=("arbitrary",),
                                vmem_limit_bytes=VMEM_LIMIT_BYTES)


def _mixer_tile(x_ref, y_ref, ubuf, vbuf, w, *, pos0):
    (gpre_ref, win_ref, wpool_ref, pscale_ref, wconv_ref, gpo_ref, gco_ref, hmean_ref, wout_ref,
     gpost_ref) = w
    nb, tm, d = x_ref.shape
    dp = ubuf.shape[-1]
    dc = vbuf.shape[-1]
    pg = dp // N_POOL_GROUPS
    m = nb * tm

    x = x_ref[...].reshape(m, d)
    xn = _rms(x, gpre_ref[...]).astype(BF16)
    proj = _dot(xn, win_ref[...])
    u = proj[:, :dp]
    bg = proj[:, dp:dp + dc]
    cg = proj[:, dp + dc:dp + 2 * dc]
    h = proj[:, dp + 2 * dc:]
    v = cg * h
    ubuf[:, POOL_HIST:POOL_HIST + tm, :] = u.reshape(nb, tm, dp)
    vbuf[:, CONV_HIST:CONV_HIST + tm, :] = v.reshape(nb, tm, dc)

    pos = pos0 + lax.broadcasted_iota(jnp.int32, (1, tm, 1), 1)
    pooled = []
    for gi, win_len in enumerate(POOL_WINDOWS):
        sl = slice(gi * pg, (gi + 1) * pg)
        cur = ubuf[:, POOL_HIST:POOL_HIST + tm, sl]
        win = cur
        for k in range(1, win_len):
            win = win + ubuf[:, POOL_HIST - k:POOL_HIST - k + tm, sl]
        inv_cnt = 1.0 / jnp.minimum(pos + 1, win_len).astype(F32)
        pooled.append((win * inv_cnt - cur).reshape(m, pg))
    mapped = []
    for j in range(N_POOL_GROUPS // 2):
        pj = jnp.concatenate(pooled[2 * j:2 * j + 2], axis=-1).astype(BF16)
        mapped.append(_dot(pj, wpool_ref[j]))
    y_pool = jnp.concatenate(mapped, axis=-1) * pscale_ref[...]
    pn = []
    for gi in range(N_POOL_GROUPS):
        sl = slice(gi * pg, (gi + 1) * pg)
        pn.append(_rms(y_pool[:, sl], gpo_ref[:, sl]))

    z = wconv_ref[0:1, :] * vbuf[:, CONV_HIST - 2:CONV_HIST - 2 + tm, :]
    z = z + wconv_ref[1:2, :] * vbuf[:, CONV_HIST - 1:CONV_HIST - 1 + tm, :]
    z = z + wconv_ref[2:3, :] * vbuf[:, CONV_HIST:CONV_HIST + tm, :]
    y_conv = bg * z.reshape(m, dc)
    ms = _dot((y_conv * y_conv).astype(BF16), hmean_ref[...])
    cn = y_conv * lax.rsqrt(ms + EPS) * gco_ref[...]

    merged = jnp.concatenate(pn + [cn], axis=-1).astype(BF16)
    y = x + _rms(_dot(merged, wout_ref[...]), gpost_ref[...])
    y_ref[...] = y.reshape(nb, tm, d)

    ubuf[:, 0:POOL_HIST, :] = ubuf[:, tm:tm + POOL_HIST, :]
    vbuf[:, 0:CONV_HIST, :] = vbuf[:, tm:tm + CONV_HIST, :]


def _mixer_kernel(xp_ref, xs_ref, sp_ref, sc_ref, *refs, n_p, nt):
    w = refs[:10]
    yp_ref, ys_ref, pstp_ref, cstp_ref, psts_ref, csts_ref, ubp, vbp, ubs, vbs = refs[10:]
    i = pl.program_id(0)
    tm = xp_ref.shape[1]
    t = i % nt

    @pl.when(i < n_p)
    def _():
        @pl.when(t == 0)
        def _():
            ubp[:, 0:POOL_HIST, :] = jnp.zeros((1, POOL_HIST, ubp.shape[-1]), F32)
            vbp[:, 0:CONV_HIST, :] = jnp.zeros((1, CONV_HIST, vbp.shape[-1]), F32)

        _mixer_tile(xp_ref, yp_ref, ubp, vbp, w, pos0=t * tm)

        @pl.when(t == nt - 1)
        def _():
            pstp_ref[...] = ubp[:, POOL_HIST - POOL_STATE:POOL_HIST, :]
            cstp_ref[...] = vbp[:, CONV_HIST - CONV_STATE:CONV_HIST, :]

    @pl.when(i == n_p)
    def _():
        ubs[:, 0:POOL_HIST - POOL_STATE, :] = jnp.zeros(
            (ubs.shape[0], POOL_HIST - POOL_STATE, ubs.shape[-1]), F32)
        ubs[:, POOL_HIST - POOL_STATE:POOL_HIST, :] = sp_ref[...]
        vbs[:, CONV_HIST - CONV_STATE:CONV_HIST, :] = sc_ref[...]
        _mixer_tile(xs_ref, ys_ref, ubs, vbs, w, pos0=PAST_LEN)
        psts_ref[...] = ubs[:, POOL_HIST - POOL_STATE:POOL_HIST, :]
        csts_ref[...] = vbs[:, CONV_HIST - CONV_STATE:CONV_HIST, :]


def _mixer(xp, xs, state_pool, state_conv, w):
    bp, sp, d = xp.shape
    bs, ss, _ = xs.shape
    dp = state_pool.shape[-1]
    dc = state_conv.shape[-1]
    tm = PROMPT_TILE
    assert sp % tm == 0 and tm % POOL_HIST == 0 and ss % POOL_HIST == 0
    nt = sp // tm
    n_p = bp * nt

    def tile(i):
        j = jnp.minimum(i, n_p - 1)
        return (j // nt, j % nt, 0)

    def stream(i):
        return (jnp.minimum(i, n_p - 1) // nt, 0, 0)

    weights = (w['g_mix_pre'], w['w_in'], w['w_pool'], w['pool_scale'], w['w_conv'],
               w['g_pool_out'], w['g_conv_out'], w['head_mean'], w['w_out'], w['g_mix_post'])
    return pl.pallas_call(
        functools.partial(_mixer_kernel, n_p=n_p, nt=nt),
        grid=(n_p + 1,),
        in_specs=[pl.BlockSpec((1, tm, d), tile), _const_spec(xs.shape),
                  _const_spec(state_pool.shape), _const_spec(state_conv.shape)]
                 + [_const_spec(a.shape) for a in weights],
        out_specs=[
            pl.BlockSpec((1, tm, d), tile), _const_spec(xs.shape),
            pl.BlockSpec((1, POOL_STATE, dp), stream), pl.BlockSpec((1, CONV_STATE, dc), stream),
            _const_spec(state_pool.shape), _const_spec(state_conv.shape),
        ],
        out_shape=[
            jax.ShapeDtypeStruct(xp.shape, F32), jax.ShapeDtypeStruct(xs.shape, F32),
            jax.ShapeDtypeStruct((bp, POOL_STATE, dp), F32),
            jax.ShapeDtypeStruct((bp, CONV_STATE, dc), F32),
            jax.ShapeDtypeStruct(state_pool.shape, F32),
            jax.ShapeDtypeStruct(state_conv.shape, F32),
        ],
        scratch_shapes=[
            pltpu.VMEM((1, POOL_HIST + tm, dp), F32), pltpu.VMEM((1, CONV_HIST + tm, dc), F32),
            pltpu.VMEM((bs, POOL_HIST + ss, dp), F32), pltpu.VMEM((bs, CONV_HIST + ss, dc), F32),
        ],
        compiler_params=_params(),
        name="mixer",
    )(xp, xs, state_pool, state_conv, *weights)


def _memkv_kernel(mem_ref, g_ref, wk_ref, wv_ref, k_ref, v_ref, kb_ref, vb_ref):
    n_heads, e = k_ref.shape[1:]
    mn = _rms(mem_ref[...], g_ref[...]).astype(BF16)
    for w_ref, o_ref, ob_ref in ((wk_ref, k_ref, kb_ref), (wv_ref, v_ref, vb_ref)):
        kv = _dot(mn, w_ref[...])
        ob_ref[...] = kv.astype(BF16)
        for hh in range(n_heads):
            o_ref[:, hh, :] = kv[:, hh * e:(hh + 1) * e]


def _memory_kv(mem, n_heads, w):
    b, n, d = mem.shape
    e = d // n_heads
    return pl.pallas_call(
        _memkv_kernel,
        grid=(b,),
        in_specs=[pl.BlockSpec((None, n, d), lambda i: (i, 0, 0)), _const_spec((1, d)),
                  _const_spec((d, d)), _const_spec((d, d))],
        out_specs=[pl.BlockSpec((None, n, n_heads, e), lambda i: (i, 0, 0, 0))] * 2
                  + [pl.BlockSpec((None, n, d), lambda i: (i, 0, 0))] * 2,
        out_shape=[jax.ShapeDtypeStruct((b, n, n_heads, e), F32)] * 2
                  + [jax.ShapeDtypeStruct((b, n, d), BF16)] * 2,
        compiler_params=_params(),
        name="memory_kv",
    )(mem, w['g_mem'], w['w_k'], w['w_v'])


def _softmax(s):
    p = jnp.exp(s - jnp.max(s, axis=-1, keepdims=True))
    return p * (1.0 / jnp.sum(p, axis=-1, keepdims=True))


def _attn_kernel(xp_ref, kp_ref, vp_ref, xs_ref, ks_ref, vs_ref, gpre_ref, wq_ref, wco_ref,
                 gpost_ref, yp_ref, ys_ref, q_s, o_s, *, n_p, n_heads):
    i = pl.program_id(0)
    d = xp_ref.shape[-1]
    e = d // n_heads
    scale = e ** -0.5

    @pl.when(i < n_p)
    def _():
        x = xp_ref[...]
        xn = _rms(x, gpre_ref[...]).astype(BF16)
        q = (_dot(xn, wq_ref[...]) * scale).astype(BF16)
        heads = []
        for hh in range(n_heads):
            sl = slice(hh * e, (hh + 1) * e)
            p = _softmax(_dot_nt(q[:, sl], kp_ref[:, sl]))
            heads.append(_dot(p.astype(BF16), vp_ref[:, sl]))
        o = jnp.concatenate(heads, axis=-1).astype(BF16)
        yp_ref[...] = x + _rms(_dot(o, wco_ref[...]), gpost_ref[...])

    @pl.when(i >= n_p)
    def _():
        g = i - n_p
        bs, ss, _ = xs_ref.shape
        nb = ks_ref.shape[0]

        @pl.when(g == 0)
        def _():
            xn = _rms(xs_ref[...].reshape(bs * ss, d), gpre_ref[...]).astype(BF16)
            q_s[...] = (_dot(xn, wq_ref[...]) * scale).astype(BF16)

        for j in range(nb):
            rows = pl.ds(pl.multiple_of((g * nb + j) * ss, ss), ss)
            for hh in range(n_heads):
                sl = slice(hh * e, (hh + 1) * e)
                p = _softmax(_dot_nt(q_s[rows, sl], ks_ref[j, :, hh, :].astype(BF16)))
                o_s[rows, sl] = _dot(p.astype(BF16), vs_ref[j, :, hh, :].astype(BF16)).astype(BF16)

        @pl.when(g == bs // nb - 1)
        def _():
            y = xs_ref[...].reshape(bs * ss, d) + _rms(_dot(o_s[...], wco_ref[...]), gpost_ref[...])
            ys_ref[...] = y.reshape(bs, ss, d)


SAMPLE_ATTN_STREAMS = 4


def _attn(xp, kp, vp, xs, ks, vs, w):
    bp, sp, d = xp.shape
    bs, ss, _ = xs.shape
    n, n_heads, e = ks.shape[1:]
    tm = PROMPT_TILE
    nb = SAMPLE_ATTN_STREAMS
    assert sp % tm == 0 and bs % nb == 0 and ss % 16 == 0
    nt = sp // tm
    n_p = bp * nt
    n_g = bs // nb

    def tile(i):
        j = jnp.minimum(i, n_p - 1)
        return (j // nt, j % nt, 0)

    def stream(i):
        return (jnp.minimum(i, n_p - 1) // nt, 0, 0)

    kvp = pl.BlockSpec((None, n, d), stream)
    kvs = pl.BlockSpec((nb, n, n_heads, e), lambda i: (jnp.clip(i - n_p, 0, n_g - 1), 0, 0, 0))
    return pl.pallas_call(
        functools.partial(_attn_kernel, n_p=n_p, n_heads=n_heads),
        grid=(n_p + n_g,),
        in_specs=[pl.BlockSpec((None, tm, d), tile), kvp, kvp, _const_spec(xs.shape), kvs, kvs,
                  _const_spec((1, d)), _const_spec((d, d)), _const_spec((d, d)),
                  _const_spec((1, d))],
        out_specs=[pl.BlockSpec((None, tm, d), tile), _const_spec(xs.shape)],
        out_shape=[jax.ShapeDtypeStruct(xp.shape, F32), jax.ShapeDtypeStruct(xs.shape, F32)],
        scratch_shapes=[pltpu.VMEM((bs * ss, d), BF16), pltpu.VMEM((bs * ss, d), BF16)],
        compiler_params=_params(),
        name="cross_attn",
    )(xp, kp, vp, xs, ks, vs, w['g_x_pre'], w['w_q'], w['w_co'], w['g_x_post'])


def _mlp_tile(x_ref, y_ref, gpre_ref, wup_ref, wdown_ref, gpost_ref):
    x = x_ref[...]
    xn = _rms(x, gpre_ref[...]).astype(BF16)
    dff = wup_ref.shape[-1]
    acc = None
    for c in range(dff // MLP_CHUNK):
        sl = slice(c * MLP_CHUNK, (c + 1) * MLP_CHUNK)
        hid = jnp.square(jnp.maximum(_dot(xn, wup_ref[:, sl]), 0.0))
        part = _dot(hid.astype(BF16), wdown_ref[sl, :])
        acc = part if acc is None else acc + part
    y_ref[...] = x + _rms(acc, gpost_ref[...])


def _mlp_kernel(xp_ref, xs_ref, gpre_ref, wup_ref, wdown_ref, gpost_ref, yp_ref, ys_ref, *, n_p):
    i = pl.program_id(0)

    @pl.when(i < n_p)
    def _():
        _mlp_tile(xp_ref, yp_ref, gpre_ref, wup_ref, wdown_ref, gpost_ref)

    @pl.when(i == n_p)
    def _():
        _mlp_tile(xs_ref, ys_ref, gpre_ref, wup_ref, wdown_ref, gpost_ref)


def _mlp(xp, xs, w):
    mp, d = xp.shape
    tm = PROMPT_TILE
    dff = w['w_up'].shape[-1]
    assert mp % tm == 0 and dff % MLP_CHUNK == 0
    n_p = mp // tm
    tile = pl.BlockSpec((tm, d), lambda i: (jnp.minimum(i, n_p - 1), 0))
    return pl.pallas_call(
        functools.partial(_mlp_kernel, n_p=n_p),
        grid=(n_p + 1,),
        in_specs=[tile, _const_spec(xs.shape), _const_spec((1, d)), _const_spec((d, dff)),
                  _const_spec((dff, d)), _const_spec((1, d))],
        out_specs=[tile, _const_spec(xs.shape)],
        out_shape=[jax.ShapeDtypeStruct(xp.shape, F32), jax.ShapeDtypeStruct(xs.shape, F32)],
        compiler_params=_params(),
        name="mlp",
    )(xp, xs, w['g_ff_pre'], w['w_up'], w['w_down'], w['g_ff_post'])


def _prep_weights(l, g_mix_pre, w_in, w_pool, pool_scale, w_conv, g_pool_out, g_conv_out, w_out,
                  g_mix_post, g_mem, w_k, w_v, g_x_pre, w_q, w_co, g_x_post, g_ff_pre, w_up,
                  w_down, g_ff_post):
    d = w_in.shape[1]
    pg = w_pool.shape[-1]
    dc = w_conv.shape[-1]
    wp = w_pool[l].astype(BF16)
    zero = jnp.zeros((pg, pg), BF16)
    wpool_bd = jnp.stack([
        jnp.block([[wp[2 * j], zero], [zero, wp[2 * j + 1]]]) for j in range(N_POOL_GROUPS // 2)])
    head = jnp.arange(dc, dtype=jnp.int32) // (dc // N_CONV_HEADS)
    head_mean = jnp.where(head[:, None] == head[None, :], N_CONV_HEADS / dc, 0.0).astype(BF16)
    row = lambda a: a[l].reshape(1, -1)
    return {
        'g_mix_pre': row(g_mix_pre), 'w_in': w_in[l].astype(BF16), 'w_pool': wpool_bd,
        'pool_scale': row(pool_scale), 'w_conv': w_conv[l], 'g_pool_out': row(g_pool_out),
        'g_conv_out': row(g_conv_out), 'head_mean': head_mean, 'w_out': w_out[l].astype(BF16),
        'g_mix_post': row(g_mix_post), 'g_mem': row(g_mem),
        'w_k': w_k[l].reshape(d, d).astype(BF16), 'w_v': w_v[l].reshape(d, d).astype(BF16),
        'g_x_pre': row(g_x_pre), 'w_q': w_q[l].reshape(d, d).astype(BF16),
        'w_co': w_co[l].reshape(d, d).astype(BF16), 'g_x_post': row(g_x_post),
        'g_ff_pre': row(g_ff_pre), 'w_up': w_up[l].astype(BF16), 'w_down': w_down[l].astype(BF16),
        'g_ff_post': row(g_ff_post),
    }


def kernel(x_prompt, x_sample, state_pool, state_conv, cache_mem_k, cache_mem_v, mem_prompt, g_mix_pre, w_in, w_pool, pool_scale, w_conv, g_pool_out, g_conv_out, w_out, g_mix_post, g_mem, w_k, w_v, g_x_pre, w_q, w_co, g_x_post, g_ff_pre, w_up, w_down, g_ff_post):
    depth = w_in.shape[0]
    bp, sp, d = x_prompt.shape
    bs, ss, _ = x_sample.shape
    n_heads = cache_mem_k.shape[3]
    yp, ys = x_prompt, x_sample
    pool_p, conv_p, mk_p, mv_p, pool_s, conv_s = [], [], [], [], [], []
    for l in range(depth):
        w = _prep_weights(l, g_mix_pre, w_in, w_pool, pool_scale, w_conv, g_pool_out, g_conv_out,
                          w_out, g_mix_post, g_mem, w_k, w_v, g_x_pre, w_q, w_co, g_x_post,
                          g_ff_pre, w_up, w_down, g_ff_post)
        mk, mv, mkb, mvb = _memory_kv(mem_prompt, n_heads, w)
        yp, ys, pstp, cstp, psts, csts = _mixer(yp, ys, state_pool[l], state_conv[l], w)
        yp, ys = _attn(yp, mkb, mvb, ys, cache_mem_k[l], cache_mem_v[l], w)
        yp, ys = _mlp(yp.reshape(bp * sp, d), ys.reshape(bs * ss, d), w)
        yp, ys = yp.reshape(bp, sp, d), ys.reshape(bs, ss, d)
        pool_p.append(pstp); conv_p.append(cstp); mk_p.append(mk); mv_p.append(mv)
        pool_s.append(psts); conv_s.append(csts)
    return (yp, ys, jnp.stack(pool_p), jnp.stack(conv_p), jnp.stack(mk_p), jnp.stack(mv_p),
            jnp.stack(pool_s), jnp.stack(conv_s))
```

```python
import functools

import jax
import jax.numpy as jnp
from jax import lax
from jax.experimental import pallas as pl
from jax.experimental.pallas import tpu as pltpu

EPS = 1e-6
POOL_WINDOWS = (2, 4, 8, 16)
N_POOL_GROUPS = len(POOL_WINDOWS)
POOL_STATE = max(POOL_WINDOWS) - 1
N_CONV_HEADS = 8
CONV_WIDTH = 3
CONV_STATE = CONV_WIDTH - 1
PAST_LEN = 2048

POOL_HIST = 16
CONV_HIST = 8

PROMPT_TILE = 512
MLP_CHUNK = 1024
VMEM_LIMIT_BYTES = 56 * 1024 * 1024

F32 = jnp.float32
BF16 = jnp.bfloat16


def _rms(x, g):
    ms = jnp.mean(x * x, axis=-1, keepdims=True)
    return x * lax.rsqrt(ms + EPS) * g


def _dot(a, b):
    return jnp.dot(a, b, preferred_element_type=F32)


def _dot_nt(a, b):
    return lax.dot_general(a, b, (((1,), (1,)), ((), ())), preferred_element_type=F32)


def _const_spec(shape):
    return pl.BlockSpec(shape, lambda *_: (0,) * len(shape))


def _params():
    return pltpu.CompilerParams(dimension_semantics=("arbitrary",),
                                vmem_limit_bytes=VMEM_LIMIT_BYTES)


def _mixer_tile(x_ref, y_ref, ubuf, vbuf, w, *, pos0):
    (gpre_ref, win_ref, wpool_ref, pscale_ref, wconv_ref, gpo_ref, gco_ref, hmean_ref, wout_ref,
     gpost_ref) = w
    nb, tm, d = x_ref.shape
    dp = ubuf.shape[-1]
    dc = vbuf.shape[-1]
    pg = dp // N_POOL_GROUPS
    m = nb * tm

    x = x_ref[...].reshape(m, d)
    xn = _rms(x, gpre_ref[...]).astype(BF16)
    proj = _dot(xn, win_ref[...])
    u = proj[:, :dp]
    bg = proj[:, dp:dp + dc]
    cg = proj[:, dp + dc:dp + 2 * dc]
    h = proj[:, dp + 2 * dc:]
    v = cg * h
    ubuf[:, POOL_HIST:POOL_HIST + tm, :] = u.reshape(nb, tm, dp)
    vbuf[:, CONV_HIST:CONV_HIST + tm, :] = v.reshape(nb, tm, dc)

    pos = pos0 + lax.broadcasted_iota(jnp.int32, (1, tm, 1), 1)
    pooled = []
    for gi, win_len in enumerate(POOL_WINDOWS):
        sl = slice(gi * pg, (gi + 1) * pg)
        cur = ubuf[:, POOL_HIST:POOL_HIST + tm, sl]
        win = cur
        for k in range(1, win_len):
            win = win + ubuf[:, POOL_HIST - k:POOL_HIST - k + tm, sl]
        inv_cnt = 1.0 / jnp.minimum(pos + 1, win_len).astype(F32)
        pooled.append((win * inv_cnt - cur).reshape(m, pg))
    mapped = []
    for j in range(N_POOL_GROUPS // 2):
        pj = jnp.concatenate(pooled[2 * j:2 * j + 2], axis=-1).astype(BF16)
        mapped.append(_dot(pj, wpool_ref[j]))
    y_pool = jnp.concatenate(mapped, axis=-1) * pscale_ref[...]
    pn = []
    for gi in range(N_POOL_GROUPS):
        sl = slice(gi * pg, (gi + 1) * pg)
        pn.append(_rms(y_pool[:, sl], gpo_ref[:, sl]))

    z = wconv_ref[0:1, :] * vbuf[:, CONV_HIST - 2:CONV_HIST - 2 + tm, :]
    z = z + wconv_ref[1:2, :] * vbuf[:, CONV_HIST - 1:CONV_HIST - 1 + tm, :]
    z = z + wconv_ref[2:3, :] * vbuf[:, CONV_HIST:CONV_HIST + tm, :]
    y_conv = bg * z.reshape(m, dc)
    ms = _dot((y_conv * y_conv).astype(BF16), hmean_ref[...])
    cn = y_conv * lax.rsqrt(ms + EPS) * gco_ref[...]

    merged = jnp.concatenate(pn + [cn], axis=-1).astype(BF16)
    y = x + _rms(_dot(merged, wout_ref[...]), gpost_ref[...])
    y_ref[...] = y.reshape(nb, tm, d)

    ubuf[:, 0:POOL_HIST, :] = ubuf[:, tm:tm + POOL_HIST, :]
    vbuf[:, 0:CONV_HIST, :] = vbuf[:, tm:tm + CONV_HIST, :]


def _mixer_kernel(xp_ref, xs_ref, sp_ref, sc_ref, *refs, n_p, nt):
    w = refs[:10]
    yp_ref, ys_ref, pstp_ref, cstp_ref, psts_ref, csts_ref, ubp, vbp, ubs, vbs = refs[10:]
    i = pl.program_id(0)
    tm = xp_ref.shape[1]
    t = i % nt

    @pl.when(i < n_p)
    def _():
        @pl.when(t == 0)
        def _():
            ubp[:, 0:POOL_HIST, :] = jnp.zeros((1, POOL_HIST, ubp.shape[-1]), F32)
            vbp[:, 0:CONV_HIST, :] = jnp.zeros((1, CONV_HIST, vbp.shape[-1]), F32)

        _mixer_tile(xp_ref, yp_ref, ubp, vbp, w, pos0=t * tm)

        @pl.when(t == nt - 1)
        def _():
            pstp_ref[...] = ubp[:, POOL_HIST - POOL_STATE:POOL_HIST, :]
            cstp_ref[...] = vbp[:, CONV_HIST - CONV_STATE:CONV_HIST, :]

    @pl.when(i == n_p)
    def _():
        ubs[:, 0:POOL_HIST - POOL_STATE, :] = jnp.zeros(
            (ubs.shape[0], POOL_HIST - POOL_STATE, ubs.shape[-1]), F32)
        ubs[:, POOL_HIST - POOL_STATE:POOL_HIST, :] = sp_ref[...]
        vbs[:, CONV_HIST - CONV_STATE:CONV_HIST, :] = sc_ref[...]
        _mixer_tile(xs_ref, ys_ref, ubs, vbs, w, pos0=PAST_LEN)
        psts_ref[...] = ubs[:, POOL_HIST - POOL_STATE:POOL_HIST, :]
        csts_ref[...] = vbs[:, CONV_HIST - CONV_STATE:CONV_HIST, :]


def _mixer(xp, xs, state_pool, state_conv, w):
    bp, sp, d = xp.shape
    bs, ss, _ = xs.shape
    dp = state_pool.shape[-1]
    dc = state_conv.shape[-1]
    tm = PROMPT_TILE
    assert sp % tm == 0 and tm % POOL_HIST == 0 and ss % POOL_HIST == 0
    nt = sp // tm
    n_p = bp * nt

    def tile(i):
        j = jnp.minimum(i, n_p - 1)
        return (j // nt, j % nt, 0)

    def stream(i):
        return (jnp.minimum(i, n_p - 1) // nt, 0, 0)

    weights = (w['g_mix_pre'], w['w_in'], w['w_pool'], w['pool_scale'], w['w_conv'],
               w['g_pool_out'], w['g_conv_out'], w['head_mean'], w['w_out'], w['g_mix_post'])
    return pl.pallas_call(
        functools.partial(_mixer_kernel, n_p=n_p, nt=nt),
        grid=(n_p + 1,),
        in_specs=[pl.BlockSpec((1, tm, d), tile), _const_spec(xs.shape),
                  _const_spec(state_pool.shape), _const_spec(state_conv.shape)]
                 + [_const_spec(a.shape) for a in weights],
        out_specs=[
            pl.BlockSpec((1, tm, d), tile), _const_spec(xs.shape),
            pl.BlockSpec((1, POOL_STATE, dp), stream), pl.BlockSpec((1, CONV_STATE, dc), stream),
            _const_spec(state_pool.shape), _const_spec(state_conv.shape),
        ],
        out_shape=[
            jax.ShapeDtypeStruct(xp.shape, F32), jax.ShapeDtypeStruct(xs.shape, F32),
            jax.ShapeDtypeStruct((bp, POOL_STATE, dp), F32),
            jax.ShapeDtypeStruct((bp, CONV_STATE, dc), F32),
            jax.ShapeDtypeStruct(state_pool.shape, F32),
            jax.ShapeDtypeStruct(state_conv.shape, F32),
        ],
        scratch_shapes=[
            pltpu.VMEM((1, POOL_HIST + tm, dp), F32), pltpu.VMEM((1, CONV_HIST + tm, dc), F32),
            pltpu.VMEM((bs, POOL_HIST + ss, dp), F32), pltpu.VMEM((bs, CONV_HIST + ss, dc), F32),
        ],
        compiler_params=_params(),
        name="mixer",
    )(xp, xs, state_pool, state_conv, *weights)


def _memkv_kernel(mem_ref, g_ref, wk_ref, wv_ref, k_ref, v_ref, kb_ref, vb_ref):
    n_heads, e = k_ref.shape[1:]
    mn = _rms(mem_ref[...], g_ref[...]).astype(BF16)
    for w_ref, o_ref, ob_ref in ((wk_ref, k_ref, kb_ref), (wv_ref, v_ref, vb_ref)):
        kv = _dot(mn, w_ref[...])
        ob_ref[...] = kv.astype(BF16)
        for hh in range(n_heads):
            o_ref[:, hh, :] = kv[:, hh * e:(hh + 1) * e]


def _memory_kv(mem, n_heads, w):
    b, n, d = mem.shape
    e = d // n_heads
    return pl.pallas_call(
        _memkv_kernel,
        grid=(b,),
        in_specs=[pl.BlockSpec((None, n, d), lambda i: (i, 0, 0)), _const_spec((1, d)),
                  _const_spec((d, d)), _const_spec((d, d))],
        out_specs=[pl.BlockSpec((None, n, n_heads, e), lambda i: (i, 0, 0, 0))] * 2
                  + [pl.BlockSpec((None, n, d), lambda i: (i, 0, 0))] * 2,
        out_shape=[jax.ShapeDtypeStruct((b, n, n_heads, e), F32)] * 2
                  + [jax.ShapeDtypeStruct((b, n, d), BF16)] * 2,
        compiler_params=_params(),
        name="memory_kv",
    )(mem, w['g_mem'], w['w_k'], w['w_v'])


def _softmax(s):
    p = jnp.exp(s - jnp.max(s, axis=-1, keepdims=True))
    return p * (1.0 / jnp.sum(p, axis=-1, keepdims=True))


def _attn_kernel(xp_ref, kp_ref, vp_ref, xs_ref, ks_hbm, vs_hbm, gpre_ref, wq_ref, wco_ref,
                 gpost_ref, yp_ref, ys_ref, q_s, o_s, kv_buf, kv_sem, *, n_p, n_heads):
    i = pl.program_id(0)
    d = xp_ref.shape[-1]
    e = d // n_heads
    scale = e ** -0.5
    bs, ss, _ = xs_ref.shape
    nb = kv_buf.shape[3]
    n_g = bs // nb

    def kv_copies(g, slot):
        return [pltpu.make_async_copy(src.at[pl.ds(g * nb, nb), :, hh, :],
                                      kv_buf.at[slot, a, hh], kv_sem.at[slot, a, hh])
                for a, src in enumerate((ks_hbm, vs_hbm)) for hh in range(n_heads)]

    @pl.when(i == 0)
    def _():
        for g in range(min(2, n_g)):
            for cp in kv_copies(g, g):
                cp.start()

    @pl.when(i < n_p)
    def _():
        x = xp_ref[...]
        xn = _rms(x, gpre_ref[...]).astype(BF16)
        q = (_dot(xn, wq_ref[...]) * scale).astype(BF16)
        heads = []
        for hh in range(n_heads):
            sl = slice(hh * e, (hh + 1) * e)
            p = _softmax(_dot_nt(q[:, sl], kp_ref[:, sl]))
            heads.append(_dot(p.astype(BF16), vp_ref[:, sl]))
        o = jnp.concatenate(heads, axis=-1).astype(BF16)
        yp_ref[...] = x + _rms(_dot(o, wco_ref[...]), gpost_ref[...])

    @pl.when(i >= n_p)
    def _():
        g = i - n_p
        slot = g % 2

        @pl.when(g == 0)
        def _():
            xn = _rms(xs_ref[...].reshape(bs * ss, d), gpre_ref[...]).astype(BF16)
            q_s[...] = (_dot(xn, wq_ref[...]) * scale).astype(BF16)

        for cp in kv_copies(g, slot):
            cp.wait()
        for j in range(nb):
            rows = pl.ds(pl.multiple_of((g * nb + j) * ss, ss), ss)
            for hh in range(n_heads):
                sl = slice(hh * e, (hh + 1) * e)
                p = _softmax(_dot_nt(q_s[rows, sl], kv_buf[slot, 0, hh, j].astype(BF16)))
                o_s[rows, sl] = _dot(p.astype(BF16), kv_buf[slot, 1, hh, j].astype(BF16)).astype(BF16)

        @pl.when(g + 2 < n_g)
        def _():
            for cp in kv_copies(g + 2, slot):
                cp.start()

        @pl.when(g == n_g - 1)
        def _():
            y = xs_ref[...].reshape(bs * ss, d) + _rms(_dot(o_s[...], wco_ref[...]), gpost_ref[...])
            ys_ref[...] = y.reshape(bs, ss, d)


SAMPLE_ATTN_STREAMS = 4


def _attn(xp, kp, vp, xs, ks, vs, w):
    bp, sp, d = xp.shape
    bs, ss, _ = xs.shape
    n, n_heads, e = ks.shape[1:]
    tm = PROMPT_TILE
    nb = SAMPLE_ATTN_STREAMS
    assert sp % tm == 0 and bs % nb == 0 and ss % 16 == 0
    nt = sp // tm
    n_p = bp * nt
    n_g = bs // nb

    def tile(i):
        j = jnp.minimum(i, n_p - 1)
        return (j // nt, j % nt, 0)

    def stream(i):
        return (jnp.minimum(i, n_p - 1) // nt, 0, 0)

    kvp = pl.BlockSpec((None, n, d), stream)
    kvs = pl.BlockSpec(memory_space=pl.ANY)
    return pl.pallas_call(
        functools.partial(_attn_kernel, n_p=n_p, n_heads=n_heads),
        grid=(n_p + n_g,),
        in_specs=[pl.BlockSpec((None, tm, d), tile), kvp, kvp, _const_spec(xs.shape), kvs, kvs,
                  _const_spec((1, d)), _const_spec((d, d)), _const_spec((d, d)),
                  _const_spec((1, d))],
        out_specs=[pl.BlockSpec((None, tm, d), tile), _const_spec(xs.shape)],
        out_shape=[jax.ShapeDtypeStruct(xp.shape, F32), jax.ShapeDtypeStruct(xs.shape, F32)],
        scratch_shapes=[pltpu.VMEM((bs * ss, d), BF16), pltpu.VMEM((bs * ss, d), BF16),
                        pltpu.VMEM((2, 2, n_heads, nb, n, e), F32),
                        pltpu.SemaphoreType.DMA((2, 2, n_heads))],
        compiler_params=_params(),
        name="cross_attn",
    )(xp, kp, vp, xs, ks, vs, w['g_x_pre'], w['w_q'], w['w_co'], w['g_x_post'])


def _mlp_tile(x_ref, y_ref, gpre_ref, wup_ref, wdown_ref, gpost_ref):
    x = x_ref[...]
    xn = _rms(x, gpre_ref[...]).astype(BF16)
    dff = wup_ref.shape[-1]
    acc = None
    for c in range(dff // MLP_CHUNK):
        sl = slice(c * MLP_CHUNK, (c + 1) * MLP_CHUNK)
        hid = jnp.square(jnp.maximum(_dot(xn, wup_ref[:, sl]), 0.0))
        part = _dot(hid.astype(BF16), wdown_ref[sl, :])
        acc = part if acc is None else acc + part
    y_ref[...] = x + _rms(acc, gpost_ref[...])


def _mlp_kernel(xp_ref, xs_ref, gpre_ref, wup_ref, wdown_ref, gpost_ref, yp_ref, ys_ref, *, n_p):
    i = pl.program_id(0)

    @pl.when(i < n_p)
    def _():
        _mlp_tile(xp_ref, yp_ref, gpre_ref, wup_ref, wdown_ref, gpost_ref)

    @pl.when(i == n_p)
    def _():
        _mlp_tile(xs_ref, ys_ref, gpre_ref, wup_ref, wdown_ref, gpost_ref)


def _mlp(xp, xs, w):
    mp, d = xp.shape
    tm = PROMPT_TILE
    dff = w['w_up'].shape[-1]
    assert mp % tm == 0 and dff % MLP_CHUNK == 0
    n_p = mp // tm
    tile = pl.BlockSpec((tm, d), lambda i: (jnp.minimum(i, n_p - 1), 0))
    return pl.pallas_call(
        functools.partial(_mlp_kernel, n_p=n_p),
        grid=(n_p + 1,),
        in_specs=[tile, _const_spec(xs.shape), _const_spec((1, d)), _const_spec((d, dff)),
                  _const_spec((dff, d)), _const_spec((1, d))],
        out_specs=[tile, _const_spec(xs.shape)],
        out_shape=[jax.ShapeDtypeStruct(xp.shape, F32), jax.ShapeDtypeStruct(xs.shape, F32)],
        compiler_params=_params(),
        name="mlp",
    )(xp, xs, w['g_ff_pre'], w['w_up'], w['w_down'], w['g_ff_post'])


def _prep_weights(l, g_mix_pre, w_in, w_pool, pool_scale, w_conv, g_pool_out, g_conv_out, w_out,
                  g_mix_post, g_mem, w_k, w_v, g_x_pre, w_q, w_co, g_x_post, g_ff_pre, w_up,
                  w_down, g_ff_post):
    d = w_in.shape[1]
    pg = w_pool.shape[-1]
    dc = w_conv.shape[-1]
    wp = w_pool[l].astype(BF16)
    zero = jnp.zeros((pg, pg), BF16)
    wpool_bd = jnp.stack([
        jnp.block([[wp[2 * j], zero], [zero, wp[2 * j + 1]]]) for j in range(N_POOL_GROUPS // 2)])
    head = jnp.arange(dc, dtype=jnp.int32) // (dc // N_CONV_HEADS)
    head_mean = jnp.where(head[:, None] == head[None, :], N_CONV_HEADS / dc, 0.0).astype(BF16)
    row = lambda a: a[l].reshape(1, -1)
    return {
        'g_mix_pre': row(g_mix_pre), 'w_in': w_in[l].astype(BF16), 'w_pool': wpool_bd,
        'pool_scale': row(pool_scale), 'w_conv': w_conv[l], 'g_pool_out': row(g_pool_out),
        'g_conv_out': row(g_conv_out), 'head_mean': head_mean, 'w_out': w_out[l].astype(BF16),
        'g_mix_post': row(g_mix_post), 'g_mem': row(g_mem),
        'w_k': w_k[l].reshape(d, d).astype(BF16), 'w_v': w_v[l].reshape(d, d).astype(BF16),
        'g_x_pre': row(g_x_pre), 'w_q': w_q[l].reshape(d, d).astype(BF16),
        'w_co': w_co[l].reshape(d, d).astype(BF16), 'g_x_post': row(g_x_post),
        'g_ff_pre': row(g_ff_pre), 'w_up': w_up[l].astype(BF16), 'w_down': w_down[l].astype(BF16),
        'g_ff_post': row(g_ff_post),
    }


def kernel(x_prompt, x_sample, state_pool, state_conv, cache_mem_k, cache_mem_v, mem_prompt, g_mix_pre, w_in, w_pool, pool_scale, w_conv, g_pool_out, g_conv_out, w_out, g_mix_post, g_mem, w_k, w_v, g_x_pre, w_q, w_co, g_x_post, g_ff_pre, w_up, w_down, g_ff_post):
    depth = w_in.shape[0]
    bp, sp, d = x_prompt.shape
    bs, ss, _ = x_sample.shape
    n_heads = cache_mem_k.shape[3]
    yp, ys = x_prompt, x_sample
    pool_p, conv_p, mk_p, mv_p, pool_s, conv_s = [], [], [], [], [], []
    for l in range(depth):
        w = _prep_weights(l, g_mix_pre, w_in, w_pool, pool_scale, w_conv, g_pool_out, g_conv_out,
                          w_out, g_mix_post, g_mem, w_k, w_v, g_x_pre, w_q, w_co, g_x_post,
                          g_ff_pre, w_up, w_down, g_ff_post)
        mk, mv, mkb, mvb = _memory_kv(mem_prompt, n_heads, w)
        yp, ys, pstp, cstp, psts, csts = _mixer(yp, ys, state_pool[l], state_conv[l], w)
        yp, ys = _attn(yp, mkb, mvb, ys, cache_mem_k[l], cache_mem_v[l], w)
        yp, ys = _mlp(yp.reshape(bp * sp, d), ys.reshape(bs * ss, d), w)
        yp, ys = yp.reshape(bp, sp, d), ys.reshape(bs, ss, d)
        pool_p.append(pstp); conv_p.append(cstp); mk_p.append(mk); mv_p.append(mv)
        pool_s.append(psts); conv_s.append(csts)
    return (yp, ys, jnp.stack(pool_p), jnp.stack(conv_p), jnp.stack(mk_p), jnp.stack(mv_p),
            jnp.stack(pool_s), jnp.stack(conv_s))
```

```python
import functools

import jax
import jax.numpy as jnp
from jax import lax
from jax.experimental import pallas as pl
from jax.experimental.pallas import tpu as pltpu

EPS = 1e-6
POOL_WINDOWS = (2, 4, 8, 16)
N_POOL_GROUPS = len(POOL_WINDOWS)
POOL_STATE = max(POOL_WINDOWS) - 1
N_CONV_HEADS = 8
CONV_WIDTH = 3
CONV_STATE = CONV_WIDTH - 1
PAST_LEN = 2048

POOL_HIST = 16
CONV_HIST = 8

PROMPT_TILE = 512
MLP_CHUNK = 1024
VMEM_LIMIT_BYTES = 56 * 1024 * 1024

F32 = jnp.float32
BF16 = jnp.bfloat16


def _rms(x, g):
    ms = jnp.mean(x * x, axis=-1, keepdims=True)
    return x * lax.rsqrt(ms + EPS) * g


def _dot(a, b):
    return jnp.dot(a, b, preferred_element_type=F32)


def _dot_nt(a, b):
    return lax.dot_general(a, b, (((1,), (1,)), ((), ())), preferred_element_type=F32)


def _const_spec(shape):
    return pl.BlockSpec(shape, lambda *_: (0,) * len(shape))


def _params():
    return pltpu.CompilerParams(dimension_semantics=("arbitrary",),
                                vmem_limit_bytes=VMEM_LIMIT_BYTES)


def _mixer_tile(x_ref, y_ref, ubuf, vbuf, w, *, pos0):
    (gpre_ref, win_ref, wpool_ref, pscale_ref, wconv_ref, gpo_ref, gco_ref, hmean_ref, wout_ref,
     gpost_ref) = w
    nb, tm, d = x_ref.shape
    dp = ubuf.shape[-1]
    dc = vbuf.shape[-1]
    pg = dp // N_POOL_GROUPS
    m = nb * tm

    x = x_ref[...].reshape(m, d)
    xn = _rms(x, gpre_ref[...]).astype(BF16)
    proj = _dot(xn, win_ref[...])
    u = proj[:, :dp]
    bg = proj[:, dp:dp + dc]
    cg = proj[:, dp + dc:dp + 2 * dc]
    h = proj[:, dp + 2 * dc:]
    v = cg * h
    ubuf[:, POOL_HIST:POOL_HIST + tm, :] = u.reshape(nb, tm, dp)
    vbuf[:, CONV_HIST:CONV_HIST + tm, :] = v.reshape(nb, tm, dc)

    pos = pos0 + lax.broadcasted_iota(jnp.int32, (1, tm, 1), 1)
    pooled = []
    for gi, win_len in enumerate(POOL_WINDOWS):
        sl = slice(gi * pg, (gi + 1) * pg)
        cur = ubuf[:, POOL_HIST:POOL_HIST + tm, sl]
        win = cur
        for k in range(1, win_len):
            win = win + ubuf[:, POOL_HIST - k:POOL_HIST - k + tm, sl]
        inv_cnt = 1.0 / jnp.minimum(pos + 1, win_len).astype(F32)
        pooled.append((win * inv_cnt - cur).reshape(m, pg))
    mapped = []
    for j in range(N_POOL_GROUPS // 2):
        pj = jnp.concatenate(pooled[2 * j:2 * j + 2], axis=-1).astype(BF16)
        mapped.append(_dot(pj, wpool_ref[j]))
    y_pool = jnp.concatenate(mapped, axis=-1) * pscale_ref[...]
    pn = []
    for gi in range(N_POOL_GROUPS):
        sl = slice(gi * pg, (gi + 1) * pg)
        pn.append(_rms(y_pool[:, sl], gpo_ref[:, sl]))

    z = wconv_ref[0:1, :] * vbuf[:, CONV_HIST - 2:CONV_HIST - 2 + tm, :]
    z = z + wconv_ref[1:2, :] * vbuf[:, CONV_HIST - 1:CONV_HIST - 1 + tm, :]
    z = z + wconv_ref[2:3, :] * vbuf[:, CONV_HIST:CONV_HIST + tm, :]
    y_conv = bg * z.reshape(m, dc)
    ms = _dot((y_conv * y_conv).astype(BF16), hmean_ref[...])
    cn = y_conv * lax.rsqrt(ms + EPS) * gco_ref[...]

    merged = jnp.concatenate(pn + [cn], axis=-1).astype(BF16)
    y = x + _rms(_dot(merged, wout_ref[...]), gpost_ref[...])
    y_ref[...] = y.reshape(nb, tm, d)

    ubuf[:, 0:POOL_HIST, :] = ubuf[:, tm:tm + POOL_HIST, :]
    vbuf[:, 0:CONV_HIST, :] = vbuf[:, tm:tm + CONV_HIST, :]


def _mixer_kernel(xp_ref, xs_ref, sp_ref, sc_ref, *refs, n_p, nt):
    w = refs[:10]
    yp_ref, ys_ref, pstp_ref, cstp_ref, psts_ref, csts_ref, ubp, vbp, ubs, vbs = refs[10:]
    i = pl.program_id(0)
    tm = xp_ref.shape[1]
    t = i % nt

    @pl.when(i < n_p)
    def _():
        @pl.when(t == 0)
        def _():
            ubp[:, 0:POOL_HIST, :] = jnp.zeros((1, POOL_HIST, ubp.shape[-1]), F32)
            vbp[:, 0:CONV_HIST, :] = jnp.zeros((1, CONV_HIST, vbp.shape[-1]), F32)

        _mixer_tile(xp_ref, yp_ref, ubp, vbp, w, pos0=t * tm)

        @pl.when(t == nt - 1)
        def _():
            pstp_ref[...] = ubp[:, POOL_HIST - POOL_STATE:POOL_HIST, :]
            cstp_ref[...] = vbp[:, CONV_HIST - CONV_STATE:CONV_HIST, :]

    @pl.when(i == n_p)
    def _():
        ubs[:, 0:POOL_HIST - POOL_STATE, :] = jnp.zeros(
            (ubs.shape[0], POOL_HIST - POOL_STATE, ubs.shape[-1]), F32)
        ubs[:, POOL_HIST - POOL_STATE:POOL_HIST, :] = sp_ref[...]
        vbs[:, CONV_HIST - CONV_STATE:CONV_HIST, :] = sc_ref[...]
        _mixer_tile(xs_ref, ys_ref, ubs, vbs, w, pos0=PAST_LEN)
        psts_ref[...] = ubs[:, POOL_HIST - POOL_STATE:POOL_HIST, :]
        csts_ref[...] = vbs[:, CONV_HIST - CONV_STATE:CONV_HIST, :]


def _mixer(xp, xs, state_pool, state_conv, w):
    bp, sp, d = xp.shape
    bs, ss, _ = xs.shape
    dp = state_pool.shape[-1]
    dc = state_conv.shape[-1]
    tm = PROMPT_TILE
    assert sp % tm == 0 and tm % POOL_HIST == 0 and ss % POOL_HIST == 0
    nt = sp // tm
    n_p = bp * nt

    def tile(i):
        j = jnp.minimum(i, n_p - 1)
        return (j // nt, j % nt, 0)

    def stream(i):
        return (jnp.minimum(i, n_p - 1) // nt, 0, 0)

    weights = (w['g_mix_pre'], w['w_in'], w['w_pool'], w['pool_scale'], w['w_conv'],
               w['g_pool_out'], w['g_conv_out'], w['head_mean'], w['w_out'], w['g_mix_post'])
    return pl.pallas_call(
        functools.partial(_mixer_kernel, n_p=n_p, nt=nt),
        grid=(n_p + 1,),
        in_specs=[pl.BlockSpec((1, tm, d), tile), _const_spec(xs.shape),
                  _const_spec(state_pool.shape), _const_spec(state_conv.shape)]
                 + [_const_spec(a.shape) for a in weights],
        out_specs=[
            pl.BlockSpec((1, tm, d), tile), _const_spec(xs.shape),
            pl.BlockSpec((1, POOL_STATE, dp), stream), pl.BlockSpec((1, CONV_STATE, dc), stream),
            _const_spec(state_pool.shape), _const_spec(state_conv.shape),
        ],
        out_shape=[
            jax.ShapeDtypeStruct(xp.shape, F32), jax.ShapeDtypeStruct(xs.shape, F32),
            jax.ShapeDtypeStruct((bp, POOL_STATE, dp), F32),
            jax.ShapeDtypeStruct((bp, CONV_STATE, dc), F32),
            jax.ShapeDtypeStruct(state_pool.shape, F32),
            jax.ShapeDtypeStruct(state_conv.shape, F32),
        ],
        scratch_shapes=[
            pltpu.VMEM((1, POOL_HIST + tm, dp), F32), pltpu.VMEM((1, CONV_HIST + tm, dc), F32),
            pltpu.VMEM((bs, POOL_HIST + ss, dp), F32), pltpu.VMEM((bs, CONV_HIST + ss, dc), F32),
        ],
        compiler_params=_params(),
        name="mixer",
    )(xp, xs, state_pool, state_conv, *weights)


def _memkv_kernel(mem_ref, g_ref, wk_ref, wv_ref, k_ref, v_ref, kbt_ref, vb_ref):
    n_heads, e = k_ref.shape[1:]
    mn = _rms(mem_ref[...], g_ref[...]).astype(BF16)
    k = _dot(mn, wk_ref[...])
    v = _dot(mn, wv_ref[...])
    vb_ref[...] = v.astype(BF16)
    for hh in range(n_heads):
        sl = slice(hh * e, (hh + 1) * e)
        k_ref[:, hh, :] = k[:, sl]
        v_ref[:, hh, :] = v[:, sl]
        kbt_ref[sl, :] = k[:, sl].T.astype(BF16)


def _memory_kv(mem, n_heads, w):
    b, n, d = mem.shape
    e = d // n_heads
    return pl.pallas_call(
        _memkv_kernel,
        grid=(b,),
        in_specs=[pl.BlockSpec((None, n, d), lambda i: (i, 0, 0)), _const_spec((1, d)),
                  _const_spec((d, d)), _const_spec((d, d))],
        out_specs=[pl.BlockSpec((None, n, n_heads, e), lambda i: (i, 0, 0, 0))] * 2
                  + [pl.BlockSpec((None, d, n), lambda i: (i, 0, 0)),
                     pl.BlockSpec((None, n, d), lambda i: (i, 0, 0))],
        out_shape=[jax.ShapeDtypeStruct((b, n, n_heads, e), F32)] * 2
                  + [jax.ShapeDtypeStruct((b, d, n), BF16), jax.ShapeDtypeStruct((b, n, d), BF16)],
        compiler_params=_params(),
        name="memory_kv",
    )(mem, w['g_mem'], w['w_k'], w['w_v'])


def _softmax(s):
    p = jnp.exp(s - jnp.max(s, axis=-1, keepdims=True))
    return p * (1.0 / jnp.sum(p, axis=-1, keepdims=True))


MASKED_SCORE = -1e30


def _attn_kernel(xp_ref, kpt_ref, vp_ref, xs_ref, ks_hbm, vs_hbm, gpre_ref, wq_ref, wco_ref,
                 gpost_ref, yp_ref, ys_ref, q_s, o_s, kv_buf, kv_sem, *, n_p, n_heads):
    i = pl.program_id(0)
    d = xp_ref.shape[-1]
    e = d // n_heads
    scale = e ** -0.5
    bs, ss, _ = xs_ref.shape
    nb = kv_buf.shape[3]
    n_g = bs // nb

    def kv_copies(g, slot):
        return [pltpu.make_async_copy(src.at[pl.ds(g * nb, nb), :, hh, :],
                                      kv_buf.at[slot, a, hh], kv_sem.at[slot, a, hh])
                for a, src in enumerate((ks_hbm, vs_hbm)) for hh in range(n_heads)]

    @pl.when(i == 0)
    def _():
        for g in range(min(2, n_g)):
            for cp in kv_copies(g, g):
                cp.start()

    @pl.when(i < n_p)
    def _():
        x = xp_ref[...]
        xn = _rms(x, gpre_ref[...]).astype(BF16)
        q = (_dot(xn, wq_ref[...]) * scale).astype(BF16)
        heads = []
        for hh in range(n_heads):
            sl = slice(hh * e, (hh + 1) * e)
            p = _softmax(_dot(q[:, sl], kpt_ref[sl, :]))
            heads.append(_dot(p.astype(BF16), vp_ref[:, sl]))
        o = jnp.concatenate(heads, axis=-1).astype(BF16)
        yp_ref[...] = x + _rms(_dot(o, wco_ref[...]), gpost_ref[...])

    @pl.when(i >= n_p)
    def _():
        g = i - n_p
        slot = g % 2
        n = kv_buf.shape[4]

        @pl.when(g == 0)
        def _():
            xn = _rms(xs_ref[...].reshape(bs * ss, d), gpre_ref[...]).astype(BF16)
            q_s[...] = (_dot(xn, wq_ref[...]) * scale).astype(BF16)

        for cp in kv_copies(g, slot):
            cp.wait()
        rows = pl.ds(pl.multiple_of(g * (nb * ss), nb * ss), nb * ss)
        own = (lax.broadcasted_iota(jnp.int32, (nb * ss, nb * n), 0) // ss
               == lax.broadcasted_iota(jnp.int32, (nb * ss, nb * n), 1) // n)
        for hh in range(n_heads):
            sl = slice(hh * e, (hh + 1) * e)
            k_all = kv_buf[slot, 0, hh].reshape(nb * n, e).astype(BF16)
            v_all = kv_buf[slot, 1, hh].reshape(nb * n, e).astype(BF16)
            p = _softmax(jnp.where(own, _dot_nt(q_s[rows, sl], k_all), MASKED_SCORE))
            o_s[rows, sl] = _dot(p.astype(BF16), v_all).astype(BF16)

        @pl.when(g + 2 < n_g)
        def _():
            for cp in kv_copies(g + 2, slot):
                cp.start()

        @pl.when(g == n_g - 1)
        def _():
            y = xs_ref[...].reshape(bs * ss, d) + _rms(_dot(o_s[...], wco_ref[...]), gpost_ref[...])
            ys_ref[...] = y.reshape(bs, ss, d)


SAMPLE_ATTN_STREAMS = 4


def _attn(xp, kpt, vp, xs, ks, vs, w):
    bp, sp, d = xp.shape
    bs, ss, _ = xs.shape
    n, n_heads, e = ks.shape[1:]
    tm = PROMPT_TILE
    nb = SAMPLE_ATTN_STREAMS
    assert sp % tm == 0 and bs % nb == 0 and ss % 16 == 0
    nt = sp // tm
    n_p = bp * nt
    n_g = bs // nb

    def tile(i):
        j = jnp.minimum(i, n_p - 1)
        return (j // nt, j % nt, 0)

    def stream(i):
        return (jnp.minimum(i, n_p - 1) // nt, 0, 0)

    kvs = pl.BlockSpec(memory_space=pl.ANY)
    return pl.pallas_call(
        functools.partial(_attn_kernel, n_p=n_p, n_heads=n_heads),
        grid=(n_p + n_g,),
        in_specs=[pl.BlockSpec((None, tm, d), tile), pl.BlockSpec((None, d, n), stream),
                  pl.BlockSpec((None, n, d), stream), _const_spec(xs.shape), kvs, kvs,
                  _const_spec((1, d)), _const_spec((d, d)), _const_spec((d, d)),
                  _const_spec((1, d))],
        out_specs=[pl.BlockSpec((None, tm, d), tile), _const_spec(xs.shape)],
        out_shape=[jax.ShapeDtypeStruct(xp.shape, F32), jax.ShapeDtypeStruct(xs.shape, F32)],
        scratch_shapes=[pltpu.VMEM((bs * ss, d), BF16), pltpu.VMEM((bs * ss, d), BF16),
                        pltpu.VMEM((2, 2, n_heads, nb, n, e), F32),
                        pltpu.SemaphoreType.DMA((2, 2, n_heads))],
        compiler_params=_params(),
        name="cross_attn",
    )(xp, kpt, vp, xs, ks, vs, w['g_x_pre'], w['w_q'], w['w_co'], w['g_x_post'])


def _mlp_tile(x_ref, y_ref, gpre_ref, wup_ref, wdown_ref, gpost_ref):
    x = x_ref[...]
    xn = _rms(x, gpre_ref[...]).astype(BF16)
    dff = wup_ref.shape[-1]
    acc = None
    for c in range(dff // MLP_CHUNK):
        sl = slice(c * MLP_CHUNK, (c + 1) * MLP_CHUNK)
        hid = jnp.square(jnp.maximum(_dot(xn, wup_ref[:, sl]), 0.0))
        part = _dot(hid.astype(BF16), wdown_ref[sl, :])
        acc = part if acc is None else acc + part
    y_ref[...] = x + _rms(acc, gpost_ref[...])


def _mlp_kernel(xp_ref, xs_ref, gpre_ref, wup_ref, wdown_ref, gpost_ref, yp_ref, ys_ref, *, n_p):
    i = pl.program_id(0)

    @pl.when(i < n_p)
    def _():
        _mlp_tile(xp_ref, yp_ref, gpre_ref, wup_ref, wdown_ref, gpost_ref)

    @pl.when(i == n_p)
    def _():
        _mlp_tile(xs_ref, ys_ref, gpre_ref, wup_ref, wdown_ref, gpost_ref)


def _mlp(xp, xs, w):
    mp, d = xp.shape
    tm = PROMPT_TILE
    dff = w['w_up'].shape[-1]
    assert mp % tm == 0 and dff % MLP_CHUNK == 0
    n_p = mp // tm
    tile = pl.BlockSpec((tm, d), lambda i: (jnp.minimum(i, n_p - 1), 0))
    return pl.pallas_call(
        functools.partial(_mlp_kernel, n_p=n_p),
        grid=(n_p + 1,),
        in_specs=[tile, _const_spec(xs.shape), _const_spec((1, d)), _const_spec((d, dff)),
                  _const_spec((dff, d)), _const_spec((1, d))],
        out_specs=[tile, _const_spec(xs.shape)],
        out_shape=[jax.ShapeDtypeStruct(xp.shape, F32), jax.ShapeDtypeStruct(xs.shape, F32)],
        compiler_params=_params(),
        name="mlp",
    )(xp, xs, w['g_ff_pre'], w['w_up'], w['w_down'], w['g_ff_post'])


def _prep_weights(l, g_mix_pre, w_in, w_pool, pool_scale, w_conv, g_pool_out, g_conv_out, w_out,
                  g_mix_post, g_mem, w_k, w_v, g_x_pre, w_q, w_co, g_x_post, g_ff_pre, w_up,
                  w_down, g_ff_post):
    d = w_in.shape[1]
    pg = w_pool.shape[-1]
    dc = w_conv.shape[-1]
    wp = w_pool[l].astype(BF16)
    zero = jnp.zeros((pg, pg), BF16)
    wpool_bd = jnp.stack([
        jnp.block([[wp[2 * j], zero], [zero, wp[2 * j + 1]]]) for j in range(N_POOL_GROUPS // 2)])
    head = jnp.arange(dc, dtype=jnp.int32) // (dc // N_CONV_HEADS)
    head_mean = jnp.where(head[:, None] == head[None, :], N_CONV_HEADS / dc, 0.0).astype(BF16)
    row = lambda a: a[l].reshape(1, -1)
    return {
        'g_mix_pre': row(g_mix_pre), 'w_in': w_in[l].astype(BF16), 'w_pool': wpool_bd,
        'pool_scale': row(pool_scale), 'w_conv': w_conv[l], 'g_pool_out': row(g_pool_out),
        'g_conv_out': row(g_conv_out), 'head_mean': head_mean, 'w_out': w_out[l].astype(BF16),
        'g_mix_post': row(g_mix_post), 'g_mem': row(g_mem),
        'w_k': w_k[l].reshape(d, d).astype(BF16), 'w_v': w_v[l].reshape(d, d).astype(BF16),
        'g_x_pre': row(g_x_pre), 'w_q': w_q[l].reshape(d, d).astype(BF16),
        'w_co': w_co[l].reshape(d, d).astype(BF16), 'g_x_post': row(g_x_post),
        'g_ff_pre': row(g_ff_pre), 'w_up': w_up[l].astype(BF16), 'w_down': w_down[l].astype(BF16),
        'g_ff_post': row(g_ff_post),
    }


def kernel(x_prompt, x_sample, state_pool, state_conv, cache_mem_k, cache_mem_v, mem_prompt, g_mix_pre, w_in, w_pool, pool_scale, w_conv, g_pool_out, g_conv_out, w_out, g_mix_post, g_mem, w_k, w_v, g_x_pre, w_q, w_co, g_x_post, g_ff_pre, w_up, w_down, g_ff_post):
    depth = w_in.shape[0]
    bp, sp, d = x_prompt.shape
    bs, ss, _ = x_sample.shape
    n_heads = cache_mem_k.shape[3]
    yp, ys = x_prompt, x_sample
    pool_p, conv_p, mk_p, mv_p, pool_s, conv_s = [], [], [], [], [], []
    for l in range(depth):
        w = _prep_weights(l, g_mix_pre, w_in, w_pool, pool_scale, w_conv, g_pool_out, g_conv_out,
                          w_out, g_mix_post, g_mem, w_k, w_v, g_x_pre, w_q, w_co, g_x_post,
                          g_ff_pre, w_up, w_down, g_ff_post)
        mk, mv, mkb, mvb = _memory_kv(mem_prompt, n_heads, w)
        yp, ys, pstp, cstp, psts, csts = _mixer(yp, ys, state_pool[l], state_conv[l], w)
        yp, ys = _attn(yp, mkb, mvb, ys, cache_mem_k[l], cache_mem_v[l], w)
        yp, ys = _mlp(yp.reshape(bp * sp, d), ys.reshape(bs * ss, d), w)
        yp, ys = yp.reshape(bp, sp, d), ys.reshape(bs, ss, d)
        pool_p.append(pstp); conv_p.append(cstp); mk_p.append(mk); mv_p.append(mv)
        pool_s.append(psts); conv_s.append(csts)
    return (yp, ys, jnp.stack(pool_p), jnp.stack(conv_p), jnp.stack(mk_p), jnp.stack(mv_p),
            jnp.stack(pool_s), jnp.stack(conv_s))
```
